```python
import math
import jax
import jax.numpy as jnp
from jax import lax
import numpy as np

D_MODEL = 1024
BATCH = 2
SEQ = 16384
DEPTH = 2

GRID_W = 64
CTX_LEN = 256
EPS = 1e-6

CONV_DIM = 512
CONV_WIDTH = 31
CONV_PAD = CONV_WIDTH // 2
DIFF_HEADS = 4
DIFF_HD = 64
DIFF_VD = 2 * DIFF_HD
DIFF_QK = DIFF_HEADS * 2 * DIFF_HD
DIFF_DIM = DIFF_HEADS * DIFF_VD
DIFF_SCALE = DIFF_HD ** -0.5
ATTN_BLOCK = 128
ROPE_BASE = 10000.0
AB_Q0 = 2 * CONV_DIM
AB_K0 = AB_Q0 + DIFF_QK
AB_V0 = AB_K0 + DIFF_QK
AB_IN = AB_V0 + DIFF_DIM

GLA_HEADS = 4
GLA_DK = D_MODEL // (2 * GLA_HEADS)
GLA_DV = D_MODEL // GLA_HEADS
GLA_KDIM = GLA_HEADS * GLA_DK
GLA_VDIM = GLA_HEADS * GLA_DV
GLA_RANK = 16
GLA_TAU = 16.0
GLA_CHUNK = 64
GLA_K0 = GLA_KDIM
GLA_V0 = 2 * GLA_KDIM
GLA_G0 = GLA_V0 + GLA_VDIM
GLA_A0 = GLA_G0 + GLA_VDIM
GLA_IN = GLA_A0 + 2 * GLA_RANK

N_GROUPS = 4
EXPERTS_PER_GROUP = 8
N_EXPERTS = N_GROUPS * EXPERTS_PER_GROUP
TOP_K = 2
D_EXPERT = 512
MOE_BLOCK = 256

kernel_name = 'hybrid_conv_diffattn_gla_hmoe_dit'


def rmsnorm(x, g):
    xf = x.astype(jnp.float32)
    y = xf * lax.rsqrt(jnp.mean(xf * xf, axis=-1, keepdims=True) + EPS)
    return (y * g.astype(jnp.float32)).astype(x.dtype)


def layernorm(x, g, b):
    xf = x.astype(jnp.float32)
    mu = jnp.mean(xf, axis=-1, keepdims=True)
    var = jnp.mean(jnp.square(xf - mu), axis=-1, keepdims=True)
    y = (xf - mu) * lax.rsqrt(var + EPS)
    return (y * g.astype(jnp.float32) + b.astype(jnp.float32)).astype(x.dtype)


def axial_rope_tables(n_lat):
    rows = n_lat // GRID_W
    r, col = jnp.meshgrid(jnp.arange(rows, dtype=jnp.float32),
                          jnp.arange(GRID_W, dtype=jnp.float32), indexing='ij')
    half = DIFF_HD // 2
    inv = ROPE_BASE ** (-jnp.arange(0, half, 2, dtype=jnp.float32) / half)
    ar = r.reshape(-1, 1) * inv
    ac = col.reshape(-1, 1) * inv
    ang = jnp.concatenate([ar, ar, ac, ac], axis=-1)
    return jnp.cos(ang), jnp.sin(ang)


def apply_rope(t, cos, sin):
    tr = t.reshape(t.shape[:-1] + (2, 2, DIFF_HD // 4))
    rot = jnp.stack([-tr[..., 1, :], tr[..., 0, :]], axis=-2).reshape(t.shape)
    cs = cos[:, None, None, :].astype(t.dtype)
    sn = sin[:, None, None, :].astype(t.dtype)
    return t * cs + rot * sn


def conformer_conv(a, g, w, b, ln_g, ln_b):
    h = a * jax.nn.sigmoid(g)
    h = lax.conv_general_dilated(h, w[:, None, :].astype(h.dtype), (1,), ((CONV_PAD, CONV_PAD),),
                                 dimension_numbers=('NWC', 'WIO', 'NWC'),
                                 feature_group_count=CONV_DIM) + b
    return jax.nn.silu(layernorm(h, ln_g, ln_b))


def conv_diffattn_mixer(nl, nc, w_in, conv_w, conv_b, ln_g, ln_b, qn_g, kn_g, lq1, lk1, lq2, lk2,
                        subln_g, w_out, lambda_init, cos, sin, need_ctx):
    bsz, n_lat, _ = nl.shape
    ul = nl @ w_in
    uc = nc @ w_in

    def qkv(u):
        b_, L = u.shape[:2]
        q = rmsnorm(u[..., AB_Q0:AB_K0].reshape(b_, L, DIFF_HEADS, 2, DIFF_HD), qn_g)
        k = rmsnorm(u[..., AB_K0:AB_V0].reshape(b_, L, DIFF_HEADS, 2, DIFF_HD), kn_g)
        v = u[..., AB_V0:].reshape(b_, L, DIFF_HEADS, DIFF_VD)
        return q, k, v

    ql, kl, vl = qkv(ul)
    qc, kc, vc = qkv(uc)
    ql = apply_rope(ql, cos, sin)
    kl = apply_rope(kl, cos, sin)
    lam = (jnp.exp(jnp.sum(lq1 * lk1).astype(jnp.float32))
           - jnp.exp(jnp.sum(lq2 * lk2).astype(jnp.float32)) + lambda_init)

    tq = lambda t: t.transpose(0, 2, 3, 1, 4)
    tv = lambda t: t.transpose(0, 2, 1, 3)

    def diff_softmax(q, k, v):
        s = jnp.einsum('bhmqd,bhmkd->bhmqk', q, k).astype(jnp.float32) * DIFF_SCALE
        p = jax.nn.softmax(s, axis=-1)
        a = p[:, :, 0] - lam * p[:, :, 1]
        return jnp.einsum('bhqk,bhkd->bhqd', a.astype(v.dtype), v)

    k_all = jnp.concatenate([tq(kc), tq(kl)], axis=3)
    v_all = jnp.concatenate([tv(vc), tv(vl)], axis=2)
    nb = n_lat // ATTN_BLOCK
    qb = tq(ql).reshape(bsz, DIFF_HEADS, 2, nb, ATTN_BLOCK, DIFF_HD).transpose(3, 0, 1, 2, 4, 5)
    ob = lax.map(lambda q: diff_softmax(q, k_all, v_all), qb)
    o_l = ob.transpose(1, 0, 3, 2, 4).reshape(bsz, n_lat, DIFF_HEADS, DIFF_VD)

    def finish(o, u):
        b_, L = u.shape[:2]
        attn = (rmsnorm(o, subln_g) * (1.0 - lambda_init)).reshape(b_, L, DIFF_DIM)
        conv = conformer_conv(u[..., :CONV_DIM], u[..., CONV_DIM:AB_Q0], conv_w, conv_b, ln_g, ln_b)
        return jnp.concatenate([conv, attn], axis=-1) @ w_out

    y_l = finish(o_l, ul)
    y_c = None
    if need_ctx:
        o_c = diff_softmax(tq(qc), tq(kc), tv(vc)).transpose(0, 2, 1, 3)
        y_c = finish(o_c, uc)
    return y_l, y_c


def gla_chunked(q, k, v, log_a, s0):
    bsz, nh, L, dk = q.shape
    dv = v.shape[-1]
    n = L // GLA_CHUNK

    def chunks(t):
        return t.astype(jnp.float32).reshape(bsz, nh, n, GLA_CHUNK, t.shape[-1]).transpose(2, 0, 1, 3, 4)

    incl = jnp.tril(jnp.ones((GLA_CHUNK, GLA_CHUNK), dtype=bool))[:, :, None]

    def step(state, inp):
        qc, kc, vc, ac = inp
        b = jnp.cumsum(ac, axis=-2)
        b_end = b[..., -1:, :]
        o_inter = jnp.einsum('bhid,bhde->bhie', qc * jnp.exp(b), state)
        decay = jnp.exp(jnp.where(incl, b[..., :, None, :] - b[..., None, :, :], -jnp.inf))
        att = jnp.einsum('bhid,bhjd,bhijd->bhij', qc, kc, decay)
        o = o_inter + jnp.einsum('bhij,bhje->bhie', att, vc)
        state = (jnp.exp(b_end[..., 0, :])[..., None] * state
                 + jnp.einsum('bhjd,bhje->bhde', kc * jnp.exp(b_end - b), vc))
        return state, o

    s_end, o = lax.scan(step, s0, (chunks(q), chunks(k), chunks(v), chunks(log_a)))
    return o.transpose(1, 2, 0, 3, 4).reshape(bsz, nh, L, dv).astype(v.dtype), s_end


def bigla_mixer(nl, nc, w_in, w_a2, b_a2, norm_g, w_out, need_ctx):
    def project(h):
        bsz, L, _ = h.shape
        u = h @ w_in
        heads = lambda t, dh: t.reshape(bsz, L, GLA_HEADS, dh).transpose(0, 2, 1, 3)
        q = heads(u[..., :GLA_K0], GLA_DK) * (GLA_DK ** -0.5)
        k = heads(u[..., GLA_K0:GLA_V0], GLA_DK)
        v = heads(u[..., GLA_V0:GLA_G0], GLA_DV)
        g = u[..., GLA_G0:GLA_A0]
        a1 = u[..., GLA_A0:].reshape(bsz, L, 2, GLA_RANK)
        z = jnp.einsum('bldr,drk->bldk', a1, w_a2) + b_a2
        log_a = jax.nn.log_sigmoid(z.astype(jnp.float32)) / GLA_TAU
        return q, k, v, g, heads(log_a[:, :, 0], GLA_DK), heads(log_a[:, :, 1], GLA_DK)

    flip = lambda t: jnp.flip(t, axis=2)
    qc, kc, vc, gc, afc, abc = project(nc)
    ql, kl, vl, gl, afl, abl = project(nl)
    s0 = jnp.zeros((nc.shape[0], GLA_HEADS, GLA_DK, GLA_DV), jnp.float32)
    o_cf, s_cf = gla_chunked(qc, kc, vc, afc, s0)
    o_cb_rev, s_cb = gla_chunked(flip(qc), flip(kc), flip(vc), flip(abc), s0)
    o_lf, _ = gla_chunked(ql, kl, vl, afl, s_cf)
    o_lb_rev, _ = gla_chunked(flip(ql), flip(kl), flip(vl), flip(abl), s_cb)

    def out(o_f, o_b_rev, g):
        bsz, _, L, _ = o_f.shape
        o = (o_f + flip(o_b_rev)).transpose(0, 2, 1, 3)
        o = rmsnorm(o, norm_g).reshape(bsz, L, GLA_VDIM)
        return (o * jax.nn.silu(g)) @ w_out

    y_l = out(o_lf, o_lb_rev, gl)
    y_c = out(o_cf, o_cb_rev, gc) if need_ctx else None
    return y_l, y_c


def hier_moe(xt, rg_w, rg_b, re_w, re_b, w1, w3, w2):
    n_tok, d = xt.shape
    lg = (xt @ rg_w).astype(jnp.float32) + rg_b
    pg = jax.nn.softmax(lg, axis=-1)
    _, gi = lax.top_k(lg, 1)
    pg_sel = jnp.take_along_axis(pg, gi, axis=-1)
    le = ((xt @ re_w).astype(jnp.float32) + re_b).reshape(n_tok, N_GROUPS, EXPERTS_PER_GROUP)
    le_sel = jnp.take_along_axis(
        le, jnp.broadcast_to(gi[:, :, None], (n_tok, 1, EXPERTS_PER_GROUP)), axis=1)[:, 0]
    top_v, top_i = lax.top_k(le_sel, TOP_K)
    gate = pg_sel * jax.nn.softmax(top_v, axis=-1)
    expert = gi * EXPERTS_PER_GROUP + top_i

    n_slot = n_tok * TOP_K
    e_flat = expert.reshape(-1)
    g_flat = gate.reshape(-1)
    tok = jnp.arange(n_slot, dtype=jnp.int32) // TOP_K
    order = jnp.argsort(e_flat)
    se = e_flat[order]
    counts = jnp.bincount(e_flat, length=N_EXPERTS)
    offs = jnp.cumsum(counts) - counts
    pcounts = (counts + MOE_BLOCK - 1) // MOE_BLOCK * MOE_BLOCK
    pends = jnp.cumsum(pcounts)
    poffs = pends - pcounts
    dest = poffs[se] + jnp.arange(n_slot, dtype=jnp.int32) - offs[se]
    n_blk = (n_slot + N_EXPERTS * (MOE_BLOCK - 1) + MOE_BLOCK - 1) // MOE_BLOCK
    n_row = n_blk * MOE_BLOCK
    row_tok = jnp.full((n_row,), n_tok, jnp.int32).at[dest].set(tok[order])
    row_gate = jnp.zeros((n_row,), jnp.float32).at[dest].set(g_flat[order])
    blk_exp = jnp.minimum(jnp.searchsorted(pends, jnp.arange(n_blk, dtype=jnp.int32) * MOE_BLOCK,
                                           side='right'), N_EXPERTS - 1)
    x_rows = jnp.concatenate([xt, jnp.zeros((1, d), xt.dtype)])[row_tok].reshape(n_blk, MOE_BLOCK, d)

    def expert_block(args):
        xb, e = args
        h = jax.nn.silu(xb @ w1[e]) * (xb @ w3[e])
        return h @ w2[e]

    y_rows = lax.map(expert_block, (x_rows, blk_exp)).reshape(n_row, d)
    y = jax.ops.segment_sum(y_rows * row_gate[:, None].astype(y_rows.dtype), row_tok,
                            num_segments=n_tok + 1)
    return y[:n_tok]


def setup_inputs(seed: int = 0) -> dict:
    key = jax.random.key(seed)
    ks = iter(jax.random.split(key, 40))

    def nrm(shape, scale):
        return jax.random.normal(next(ks), shape, jnp.float32) * scale

    def gain(shape):
        return 1.0 + nrm(shape, 0.02)

    ne, no = (DEPTH + 1) // 2, DEPTH // 2
    d = D_MODEL
    return {
        'x': nrm((BATCH, SEQ, d), 1.0),
        'c': nrm((BATCH, d), 1.0),
        'ctx': nrm((BATCH, CTX_LEN, d), 1.0),
        'c_ctx': nrm((d,), 1.0),
        'ada_w': nrm((DEPTH, d, 6 * d), 0.5 * d ** -0.5),
        'ada_b': nrm((DEPTH, 6 * d), 0.01),
        'norm1_g': gain((DEPTH, d)),
        'norm2_g': gain((DEPTH, d)),
        'ab_w_in': nrm((ne, d, AB_IN), d ** -0.5),
        'conv_w': nrm((ne, CONV_WIDTH, CONV_DIM), CONV_WIDTH ** -0.5),
        'conv_b': nrm((ne, CONV_DIM), 0.01),
        'conv_ln_g': gain((ne, CONV_DIM)),
        'conv_ln_b': nrm((ne, CONV_DIM), 0.01),
        'diff_qnorm_g': gain((ne, DIFF_HD)),
        'diff_knorm_g': gain((ne, DIFF_HD)),
        'diff_lq1': nrm((ne, DIFF_HD), 0.1),
        'diff_lk1': nrm((ne, DIFF_HD), 0.1),
        'diff_lq2': nrm((ne, DIFF_HD), 0.1),
        'diff_lk2': nrm((ne, DIFF_HD), 0.1),
        'diff_subln_g': gain((ne, DIFF_VD)),
        'ab_w_out': nrm((ne, CONV_DIM + DIFF_DIM, d), (CONV_DIM + DIFF_DIM) ** -0.5),
        'gla_w_in': nrm((no, d, GLA_IN), d ** -0.5),
        'gla_w_a2': nrm((no, 2, GLA_RANK, GLA_KDIM), GLA_RANK ** -0.5),
        'gla_b_a2': nrm((no, 2, GLA_KDIM), 0.1),
        'gla_norm_g': gain((no, GLA_DV)),
        'gla_w_out': nrm((no, GLA_VDIM, d), GLA_VDIM ** -0.5),
        'rg_w': nrm((DEPTH, d, N_GROUPS), d ** -0.5),
        'rg_b': nrm((DEPTH, N_GROUPS), 0.01),
        're_w': nrm((DEPTH, d, N_EXPERTS), d ** -0.5),
        're_b': nrm((DEPTH, N_EXPERTS), 0.01),
        'moe_w1': nrm((DEPTH, N_EXPERTS, d, D_EXPERT), d ** -0.5),
        'moe_w3': nrm((DEPTH, N_EXPERTS, d, D_EXPERT), d ** -0.5),
        'moe_w2': nrm((DEPTH, N_EXPERTS, D_EXPERT, d), D_EXPERT ** -0.5),
    }


def reference(x, c, ctx, c_ctx, ada_w, ada_b, norm1_g, norm2_g, ab_w_in, conv_w, conv_b, conv_ln_g,
              conv_ln_b, diff_qnorm_g, diff_knorm_g, diff_lq1, diff_lk1, diff_lq2, diff_lk2,
              diff_subln_g, ab_w_out, gla_w_in, gla_w_a2, gla_b_a2, gla_norm_g, gla_w_out,
              rg_w, rg_b, re_w, re_b, moe_w1, moe_w3, moe_w2):
    bsz, n_lat, d = x.shape
    cos, sin = axial_rope_tables(n_lat)
    s_lat = jax.nn.silu(c)
    s_ctx = jax.nn.silu(c_ctx)
    hl, hc = x, ctx
    for l in range(DEPTH):
        last = l == DEPTH - 1
        i = l // 2
        ml = jnp.split((s_lat @ ada_w[l] + ada_b[l])[:, None, :], 6, axis=-1)
        mc = jnp.split(s_ctx @ ada_w[l] + ada_b[l], 6, axis=-1)
        nl = rmsnorm(hl, norm1_g[l]) * (1 + ml[1]) + ml[0]
        nc = rmsnorm(hc, norm1_g[l]) * (1 + mc[1]) + mc[0]
        if l % 2 == 0:
            lambda_init = 0.8 - 0.6 * math.exp(-0.3 * l)
            yl, yc = conv_diffattn_mixer(nl, nc, ab_w_in[i], conv_w[i], conv_b[i], conv_ln_g[i],
                                         conv_ln_b[i], diff_qnorm_g[i], diff_knorm_g[i],
                                         diff_lq1[i], diff_lk1[i], diff_lq2[i], diff_lk2[i],
                                         diff_subln_g[i], ab_w_out[i], lambda_init, cos, sin,
                                         not last)
        else:
            yl, yc = bigla_mixer(nl, nc, gla_w_in[i], gla_w_a2[i], gla_b_a2[i], gla_norm_g[i],
                                 gla_w_out[i], not last)
        hl = hl + ml[2] * yl
        nl = rmsnorm(hl, norm2_g[l]) * (1 + ml[4]) + ml[3]
        if last:
            y = hier_moe(nl.reshape(-1, d), rg_w[l], rg_b[l], re_w[l], re_b[l],
                         moe_w1[l], moe_w3[l], moe_w2[l])
            hl = hl + ml[5] * y.reshape(hl.shape)
        else:
            hc = hc + mc[2] * yc
            nc = rmsnorm(hc, norm2_g[l]) * (1 + mc[4]) + mc[3]
            n_l = bsz * n_lat
            y = hier_moe(jnp.concatenate([nl.reshape(-1, d), nc.reshape(-1, d)], axis=0),
                         rg_w[l], rg_b[l], re_w[l], re_b[l], moe_w1[l], moe_w3[l], moe_w2[l])
            hl = hl + ml[5] * y[:n_l].reshape(hl.shape)
            hc = hc + mc[5] * y[n_l:].reshape(hc.shape)
    return hl
```

```python
import functools
import math

import jax
import jax.numpy as jnp
from jax import lax
from jax.experimental import pallas as pl
from jax.experimental.pallas import tpu as pltpu

F32 = jnp.float32
BF16 = jnp.bfloat16

EPS = 1e-6
GRID_W = 64
ROPE_BASE = 10000.0
GLA_TAU = 16.0
TOP_K = 2
MOE_BLOCK = 256
GLA_BLOCK = 128
GLA_SAFE_DECAY = 60.0
HALO = 16
LANES = 128
VMEM_LIMIT = 56 * 1024 * 1024

_NT = (((1,), (1,)), ((), ()))


def _dot(a, b):
    return jnp.dot(a, b, preferred_element_type=F32)


def _dot_nt(a, b):
    return lax.dot_general(a, b, _NT, preferred_element_type=F32)


def _split_bf16(x):
    hi = x.astype(BF16)
    lo = (x - hi.astype(F32)).astype(BF16)
    return hi, lo


def _sigmoid(x):
    return 1.0 / (1.0 + jnp.exp(-x))


def _silu(x):
    return x * _sigmoid(x)


def _norm_mod(h, g, shift, scale):
    ms = jnp.mean(h * h, axis=-1, keepdims=True)
    return h * lax.rsqrt(ms + EPS) * g * (1.0 + scale) + shift


def _params(sem):
    return pltpu.CompilerParams(dimension_semantics=sem, vmem_limit_bytes=VMEM_LIMIT)


def _adaln_kernel(c_ref, w_ref, b_ref, o_ref):
    s = _silu(c_ref[...])
    s_hi, s_lo = _split_bf16(s)
    w_hi, w_lo = _split_bf16(w_ref[0])
    o_ref[0] = _dot(s_hi, w_hi) + _dot(s_lo, w_hi) + _dot(s_hi, w_lo) + b_ref[0]


def _adaln(cvec, ada_w, ada_b):
    depth, d, n = ada_w.shape
    tn = n // 4
    return pl.pallas_call(
        _adaln_kernel,
        grid=(depth, n // tn),
        in_specs=[pl.BlockSpec((8, d), lambda l, j: (0, 0)),
                  pl.BlockSpec((1, d, tn), lambda l, j: (l, 0, j)),
                  pl.BlockSpec((1, 1, tn), lambda l, j: (l, 0, j))],
        out_specs=pl.BlockSpec((1, 8, tn), lambda l, j: (l, 0, j)),
        out_shape=jax.ShapeDtypeStruct((depth, 8, n), F32),
        compiler_params=_params(("arbitrary", "arbitrary")),
        name="adaln",
    )(cvec, ada_w, ada_b.reshape(depth, 1, n))


def _even_in_kernel(h_ref, mod_ref, g1_ref, w_ref, cos_ref, sa_ref, sb_ref, qg_ref, kg_ref, gm_ref,
                    hc_ref, q_ref, k_ref, v_ref, *, conv_dim, qk_dim, scale):
    nl = _norm_mod(h_ref[...], g1_ref[...], mod_ref[0, 0:1, :], mod_ref[0, 1:2, :]).astype(BF16)
    c, q = conv_dim, qk_dim
    ua = _dot(nl, w_ref[:, 0:c])
    ug = _dot(nl, w_ref[:, c:2 * c])
    hc_ref[...] = ua * _sigmoid(ug)
    gm = gm_ref[...]
    cos, sa, sb = cos_ref[...], sa_ref[...], sb_ref[...]

    def qk_norm_rope(u, g):
        hi, lo = _split_bf16(u * u)
        ms = _dot(hi, gm) + _dot(lo, gm)
        n = u * lax.rsqrt(ms + EPS) * g
        return n * cos + pltpu.roll(n, q - 16, 1) * sa + pltpu.roll(n, 16, 1) * sb

    uq = _dot(nl, w_ref[:, 2 * c:2 * c + q])
    q_ref[...] = (qk_norm_rope(uq, qg_ref[...]) * scale).astype(BF16)
    uk = _dot(nl, w_ref[:, 2 * c + q:2 * c + 2 * q])
    k_ref[...] = qk_norm_rope(uk, kg_ref[...]).astype(BF16)
    v_ref[...] = _dot(nl, w_ref[:, 2 * c + 2 * q:]).astype(BF16)


def _even_in(h, mods, g1, w_in, tables, qg, kg, gm, *, tm, midx, tidx, conv_dim, qk_dim, scale):
    r, d = h.shape
    n_in = w_in.shape[1]
    vd = n_in - 2 * conv_dim - 2 * qk_dim
    cos, sa, sb = tables
    row = lambda i: (i, 0)
    const = lambda i: (0, 0)
    tab = lambda i: (tidx(i), 0)
    return pl.pallas_call(
        functools.partial(_even_in_kernel, conv_dim=conv_dim, qk_dim=qk_dim, scale=scale),
        grid=(r // tm,),
        in_specs=[pl.BlockSpec((tm, d), row),
                  pl.BlockSpec((1, 6, d), lambda i: (midx(i), 0, 0)),
                  pl.BlockSpec((1, d), const),
                  pl.BlockSpec((d, n_in), const),
                  pl.BlockSpec((tm, qk_dim), tab),
                  pl.BlockSpec((tm, qk_dim), tab),
                  pl.BlockSpec((tm, qk_dim), tab),
                  pl.BlockSpec((1, qk_dim), const),
                  pl.BlockSpec((1, qk_dim), const),
                  pl.BlockSpec((qk_dim, qk_dim), const)],
        out_specs=[pl.BlockSpec((tm, conv_dim), row),
                   pl.BlockSpec((tm, qk_dim), row),
                   pl.BlockSpec((tm, qk_dim), row),
                   pl.BlockSpec((tm, vd), row)],
        out_shape=[jax.ShapeDtypeStruct((r, conv_dim), F32),
                   jax.ShapeDtypeStruct((r, qk_dim), BF16),
                   jax.ShapeDtypeStruct((r, qk_dim), BF16),
                   jax.ShapeDtypeStruct((r, vd), BF16)],
        compiler_params=_params(("arbitrary",)),
        name="even_in",
    )(h, mods, g1, w_in, cos, sa, sb, qg, kg, gm)


def _attn_kernel(lam_ref, q_ref, k_ref, v_ref, sg_ref, o_ref, m_sc, l_sc, acc_sc, *, tk, nk, hd, post):
    q = q_ref[0].astype(F32)
    lane = lax.broadcasted_iota(jnp.int32, q.shape, 1)
    qm = (jnp.where(lane < hd, q, 0.0).astype(BF16), jnp.where(lane >= hd, q, 0.0).astype(BF16))
    m_sc[...] = jnp.full(m_sc.shape, -jnp.inf, F32)
    l_sc[...] = jnp.zeros(l_sc.shape, F32)
    acc_sc[...] = jnp.zeros(acc_sc.shape, F32)

    def body(j, carry):
        off = pl.multiple_of(j * tk, tk)
        kc = k_ref[0, pl.ds(off, tk), :]
        vc = v_ref[0, pl.ds(off, tk), :]
        for m in range(2):
            s = _dot_nt(qm[m], kc)
            m_prev = m_sc[m]
            m_new = jnp.maximum(m_prev, jnp.max(s, axis=-1, keepdims=True))
            alpha = jnp.exp(m_prev - m_new)
            p = jnp.exp(s - m_new)
            l_sc[m] = alpha * l_sc[m] + jnp.sum(p, axis=-1, keepdims=True)
            acc_sc[m] = alpha * acc_sc[m] + _dot(p.astype(BF16), vc)
            m_sc[m] = m_new
        return carry

    lax.fori_loop(0, nk, body, 0)
    o = acc_sc[0] / l_sc[0] - lam_ref[...] * (acc_sc[1] / l_sc[1])
    ms = jnp.mean(o * o, axis=-1, keepdims=True)
    o_ref[0] = (o * lax.rsqrt(ms + EPS) * sg_ref[...] * post).astype(BF16)


def _attention(q, k, v, lam, sg, *, heads, hd, post):
    b, lq, _ = q.shape
    lk = k.shape[1]
    vd = v.shape[2] // heads
    tq = 512 if lq % 512 == 0 else lq
    tk = 256 if lk % 256 == 0 else LANES
    return pl.pallas_call(
        functools.partial(_attn_kernel, tk=tk, nk=lk // tk, hd=hd, post=post),
        grid=(b, heads, lq // tq),
        in_specs=[pl.BlockSpec((1, 1), lambda bi, h, i: (0, 0)),
                  pl.BlockSpec((1, tq, 2 * hd), lambda bi, h, i: (bi, i, h)),
                  pl.BlockSpec((1, lk, 2 * hd), lambda bi, h, i: (bi, 0, h)),
                  pl.BlockSpec((1, lk, vd), lambda bi, h, i: (bi, 0, h)),
                  pl.BlockSpec((1, vd), lambda bi, h, i: (0, 0))],
        out_specs=pl.BlockSpec((1, tq, vd), lambda bi, h, i: (bi, i, h)),
        out_shape=jax.ShapeDtypeStruct((b, lq, heads * vd), BF16),
        scratch_shapes=[pltpu.VMEM((2, tq, 1), F32), pltpu.VMEM((2, tq, 1), F32),
                        pltpu.VMEM((2, tq, vd), F32)],
        compiler_params=_params(("arbitrary", "arbitrary", "arbitrary")),
        name="diff_attn",
    )(lam, q, k, v, sg)


def _conv_kernel(flag_ref, cur_ref, prev_ref, next_ref, w_ref, b_ref, lg_ref, lb_ref, o_ref, buf, *, tc, width):
    i = pl.program_id(0)
    first = flag_ref[0, i] == 1
    last = flag_ref[1, i] == 1
    buf[0:HALO, :] = jnp.where(first, 0.0, prev_ref[...])
    buf[HALO:HALO + tc, :] = cur_ref[...]
    buf[HALO + tc:, :] = jnp.where(last, 0.0, next_ref[...])
    base = HALO - width // 2
    acc = buf[pl.ds(base, tc), :] * w_ref[0:1, :]
    for t in range(1, width):
        acc = acc + buf[pl.ds(base + t, tc), :] * w_ref[t:t + 1, :]
    acc = acc + b_ref[...]
    mu = jnp.mean(acc, axis=-1, keepdims=True)
    xc = acc - mu
    var = jnp.mean(xc * xc, axis=-1, keepdims=True)
    y = xc * lax.rsqrt(var + EPS) * lg_ref[...] + lb_ref[...]
    o_ref[...] = _silu(y).astype(BF16)


def _conv(hc, flags, w, b, lg, lb, *, tc):
    r, c = hc.shape
    width = w.shape[0]
    nh = tc // HALO
    grid_spec = pltpu.PrefetchScalarGridSpec(
        num_scalar_prefetch=1,
        grid=(r // tc,),
        in_specs=[pl.BlockSpec((tc, c), lambda i, f: (i, 0)),
                  pl.BlockSpec((HALO, c), lambda i, f: (jnp.maximum(i * nh - 1, 0), 0)),
                  pl.BlockSpec((HALO, c), lambda i, f: (jnp.minimum((i + 1) * nh, r // HALO - 1), 0)),
                  pl.BlockSpec((width, c), lambda i, f: (0, 0)),
                  pl.BlockSpec((1, c), lambda i, f: (0, 0)),
                  pl.BlockSpec((1, c), lambda i, f: (0, 0)),
                  pl.BlockSpec((1, c), lambda i, f: (0, 0))],
        out_specs=pl.BlockSpec((tc, c), lambda i, f: (i, 0)),
        scratch_shapes=[pltpu.VMEM((tc + 2 * HALO, c), F32)])
    return pl.pallas_call(
        functools.partial(_conv_kernel, tc=tc, width=width),
        grid_spec=grid_spec,
        out_shape=jax.ShapeDtypeStruct((r, c), BF16),
        compiler_params=_params(("arbitrary",)),
        name="conformer_conv",
    )(flags, hc, hc, hc, w, b, lg, lb)


def _route(nl2, wr_hi_ref, wr_lo_ref, rb_ref, *, n_groups, per_group):
    x_hi, x_lo = _split_bf16(nl2)
    wr_hi = wr_hi_ref[...]
    logits = _dot(x_hi, wr_hi) + _dot(x_lo, wr_hi) + _dot(x_hi, wr_lo_ref[...]) + rb_ref[...]
    lane = lax.broadcasted_iota(jnp.int32, logits.shape, 1)
    ninf = -jnp.inf
    big = jnp.int32(1 << 20)
    lg = jnp.where(lane < n_groups, logits, ninf)
    gmax = jnp.max(lg, axis=-1, keepdims=True)
    gi = jnp.min(jnp.where(lg == gmax, lane, big), axis=-1, keepdims=True)
    pg_sel = 1.0 / jnp.sum(jnp.exp(lg - gmax), axis=-1, keepdims=True)
    lo_lane = n_groups + gi * per_group
    le = jnp.where(lane >= lo_lane, jnp.where(lane < lo_lane + per_group, logits, ninf), ninf)
    v1 = jnp.max(le, axis=-1, keepdims=True)
    i1 = jnp.min(jnp.where(le == v1, lane, big), axis=-1, keepdims=True)
    le2 = jnp.where(lane == i1, ninf, le)
    v2 = jnp.max(le2, axis=-1, keepdims=True)
    i2 = jnp.min(jnp.where(le2 == v2, lane, big), axis=-1, keepdims=True)
    t = jnp.exp(v2 - v1)
    g1 = pg_sel / (1.0 + t)
    g2 = pg_sel * t / (1.0 + t)
    e1 = (i1 - n_groups).astype(F32)
    e2 = (i2 - n_groups).astype(F32)
    return jnp.where(lane == 0, e1, jnp.where(lane == 1, e2, jnp.where(lane == 2, g1, jnp.where(lane == 3, g2, 0.0))))


def _finish_tile(y, h_ref, mod_ref, g2_ref, wr_hi_ref, wr_lo_ref, rb_ref, hn_ref, xt_ref, r_ref, route_kw):
    hn = h_ref[...] + mod_ref[0, 2:3, :] * y
    hn_ref[...] = hn
    nl2 = _norm_mod(hn, g2_ref[...], mod_ref[0, 3:4, :], mod_ref[0, 4:5, :])
    xt_ref[...] = nl2
    r_ref[...] = _route(nl2, wr_hi_ref, wr_lo_ref, rb_ref, **route_kw)


def _even_out_kernel(conv_ref, attn_ref, h_ref, mod_ref, wo_ref, g2_ref, wr_hi_ref, wr_lo_ref, rb_ref,
                     hn_ref, xt_ref, r_ref, *, conv_dim, route_kw):
    y = _dot(conv_ref[...], wo_ref[0:conv_dim, :]) + _dot(attn_ref[...], wo_ref[conv_dim:, :])
    _finish_tile(y, h_ref, mod_ref, g2_ref, wr_hi_ref, wr_lo_ref, rb_ref, hn_ref, xt_ref, r_ref, route_kw)


def _odd_out_kernel(of_ref, ob_ref, gate_ref, ng_ref, h_ref, mod_ref, wo_ref, g2_ref, wr_hi_ref, wr_lo_ref, rb_ref,
                    hn_ref, xt_ref, r_ref, *, heads, route_kw):
    o = of_ref[...].astype(F32) + ob_ref[...].astype(F32)
    dv = o.shape[1] // heads
    parts = []
    for hh in range(heads):
        oh = o[:, hh * dv:(hh + 1) * dv]
        ms = jnp.mean(oh * oh, axis=-1, keepdims=True)
        parts.append(oh * lax.rsqrt(ms + EPS) * ng_ref[...])
    on = jnp.concatenate(parts, axis=1) * _silu(gate_ref[...].astype(F32))
    y = _dot(on.astype(BF16), wo_ref[...])
    _finish_tile(y, h_ref, mod_ref, g2_ref, wr_hi_ref, wr_lo_ref, rb_ref, hn_ref, xt_ref, r_ref, route_kw)


def _call_out(kernel_fn, row_acts, pre_consts, h, mods, post_consts, *, tm, n_tiles, midx, name):
    d = h.shape[1]
    r_out = n_tiles * tm
    row = lambda i: (i, 0)
    const2 = lambda i: (0, 0)
    in_specs = ([pl.BlockSpec((tm, a.shape[1]), row) for a in row_acts]
                + [pl.BlockSpec(a.shape, const2) for a in pre_consts]
                + [pl.BlockSpec((tm, d), row), pl.BlockSpec((1, 6, d), lambda i: (midx(i), 0, 0))]
                + [pl.BlockSpec(a.shape, const2) for a in post_consts])
    return pl.pallas_call(
        kernel_fn,
        grid=(n_tiles,),
        in_specs=in_specs,
        out_specs=[pl.BlockSpec((tm, d), row), pl.BlockSpec((tm, d), row), pl.BlockSpec((tm, LANES), row)],
        out_shape=[jax.ShapeDtypeStruct((r_out, d), F32), jax.ShapeDtypeStruct((r_out, d), F32),
                   jax.ShapeDtypeStruct((r_out, LANES), F32)],
        compiler_params=_params(("arbitrary",)),
        name=name,
    )(*row_acts, *pre_consts, h, mods, *post_consts)


def _gather_rows(idx_ref, n, src_hbm, dst, sem):
    def body(r, carry):
        pltpu.make_async_copy(src_hbm.at[pl.ds(idx_ref[0, 0, r], 1)], dst.at[pl.ds(r, 1)], sem).start()
        return carry
    lax.fori_loop(0, n, body, 0)


def _wait_rows(n, src_hbm, dst, sem):
    pltpu.make_async_copy(src_hbm.at[pl.ds(0, n)], dst, sem).wait()


def _expert_kernel(bexp_ref, tok_ref, tokn_ref, xt_hbm, w1_ref, w3_ref, w2_ref, y_ref, xbuf, sem, *, blk):
    i = pl.program_id(0)
    n = pl.num_programs(0)
    slot = lax.rem(i, 2)

    @pl.when(i == 0)
    def _():
        _gather_rows(tok_ref, blk, xt_hbm, xbuf.at[0], sem.at[0])

    @pl.when(i + 1 < n)
    def _():
        _gather_rows(tokn_ref, blk, xt_hbm, xbuf.at[1 - slot], sem.at[1 - slot])

    _wait_rows(blk, xt_hbm, xbuf.at[slot], sem.at[slot])
    xb = xbuf[slot].astype(BF16)
    hid = _silu(_dot(xb, w1_ref[0])) * _dot(xb, w3_ref[0])
    y_ref[...] = _dot(hid.astype(BF16), w2_ref[0])


def _experts(blk_exp, row_tok, xt, w1, w3, w2):
    n_blk = blk_exp.shape[0]
    blk = MOE_BLOCK
    d = xt.shape[1]
    de = w1.shape[2]
    tok3 = row_tok.reshape(n_blk, 1, blk)
    grid_spec = pltpu.PrefetchScalarGridSpec(
        num_scalar_prefetch=1,
        grid=(n_blk,),
        in_specs=[pl.BlockSpec((1, 1, blk), lambda i, be: (i, 0, 0), memory_space=pltpu.SMEM),
                  pl.BlockSpec((1, 1, blk), lambda i, be: (jnp.minimum(i + 1, n_blk - 1), 0, 0),
                               memory_space=pltpu.SMEM),
                  pl.BlockSpec(memory_space=pl.ANY),
                  pl.BlockSpec((1, d, de), lambda i, be: (be[i], 0, 0)),
                  pl.BlockSpec((1, d, de), lambda i, be: (be[i], 0, 0)),
                  pl.BlockSpec((1, de, d), lambda i, be: (be[i], 0, 0))],
        out_specs=pl.BlockSpec((blk, d), lambda i, be: (i, 0)),
        scratch_shapes=[pltpu.VMEM((2, blk, d), F32), pltpu.SemaphoreType.DMA((2,))])
    return pl.pallas_call(
        functools.partial(_expert_kernel, blk=blk),
        grid_spec=grid_spec,
        out_shape=jax.ShapeDtypeStruct((n_blk * blk, d), F32),
        compiler_params=_params(("arbitrary",)),
        name="moe_experts",
    )(blk_exp, tok3, tok3, xt, w1, w3, w2)


def _combine_kernel(pos_ref, posn_ref, y_hbm, h_ref, r_ref, mod_ref, o_ref, ybuf, sem, *, tm):
    i = pl.program_id(0)
    n = pl.num_programs(0)
    slot = lax.rem(i, 2)

    @pl.when(i == 0)
    def _():
        _gather_rows(pos_ref, 2 * tm, y_hbm, ybuf.at[0], sem.at[0])

    @pl.when(i + 1 < n)
    def _():
        _gather_rows(posn_ref, 2 * tm, y_hbm, ybuf.at[1 - slot], sem.at[1 - slot])

    _wait_rows(2 * tm, y_hbm, ybuf.at[slot], sem.at[slot])
    r = r_ref[...]
    y = r[:, 2:3] * ybuf[slot, 0:tm, :] + r[:, 3:4] * ybuf[slot, tm:, :]
    o_ref[...] = h_ref[...] + mod_ref[0, 5:6, :] * y


def _combine(pos, y_rows, h, route, mods, *, tm, midx):
    r, d = h.shape
    n_t = r // tm
    pos3 = pos.reshape(n_t, tm, TOP_K).transpose(0, 2, 1).reshape(n_t, 1, TOP_K * tm)
    return pl.pallas_call(
        functools.partial(_combine_kernel, tm=tm),
        grid=(n_t,),
        in_specs=[pl.BlockSpec((1, 1, 2 * tm), lambda i: (i, 0, 0), memory_space=pltpu.SMEM),
                  pl.BlockSpec((1, 1, 2 * tm), lambda i: (jnp.minimum(i + 1, n_t - 1), 0, 0),
                               memory_space=pltpu.SMEM),
                  pl.BlockSpec(memory_space=pl.ANY),
                  pl.BlockSpec((tm, d), lambda i: (i, 0)),
                  pl.BlockSpec((tm, LANES), lambda i: (i, 0)),
                  pl.BlockSpec((1, 6, d), lambda i: (midx(i), 0, 0))],
        out_specs=pl.BlockSpec((tm, d), lambda i: (i, 0)),
        out_shape=jax.ShapeDtypeStruct((r, d), F32),
        scratch_shapes=[pltpu.VMEM((2, 2 * tm, d), F32), pltpu.SemaphoreType.DMA((2,))],
        compiler_params=_params(("arbitrary",)),
        name="moe_combine",
    )(pos3, pos3, y_rows, h, route, mods)


def _dispatch(route, n_experts):
    n_tok = route.shape[0]
    n_slot = n_tok * TOP_K
    e_flat = route[:, 0:TOP_K].astype(jnp.int32).reshape(-1)
    order = jnp.argsort(e_flat)
    se = e_flat[order]
    counts = jnp.bincount(e_flat, length=n_experts)
    offs = jnp.cumsum(counts) - counts
    pcounts = (counts + MOE_BLOCK - 1) // MOE_BLOCK * MOE_BLOCK
    pends = jnp.cumsum(pcounts)
    poffs = pends - pcounts
    dest = (poffs[se] + jnp.arange(n_slot, dtype=jnp.int32) - offs[se]).astype(jnp.int32)
    n_blk = (n_slot + n_experts * (MOE_BLOCK - 1) + MOE_BLOCK - 1) // MOE_BLOCK
    tok = (order // TOP_K).astype(jnp.int32)
    row_tok = jnp.zeros((n_blk * MOE_BLOCK,), jnp.int32).at[dest].set(tok)
    pos = jnp.zeros((n_slot,), jnp.int32).at[order].set(dest)
    blk_exp = jnp.minimum(jnp.searchsorted(pends, jnp.arange(n_blk, dtype=jnp.int32) * MOE_BLOCK, side='right'),
                          n_experts - 1).astype(jnp.int32)
    return blk_exp, row_tok, pos


def _moe(h, xt, route, mods, w1, w3, w2, *, tm, midx):
    blk_exp, row_tok, pos = _dispatch(route, w1.shape[0])
    y_rows = _experts(blk_exp, row_tok, xt, w1, w3, w2)
    return _combine(pos, y_rows, h, route, mods, tm=tm, midx=midx)


def _odd_in_kernel(h_ref, mod_ref, g1_ref, w_ref, wa_ref, w2_ref, b2_ref,
                   q_ref, k_ref, v_ref, g_ref, la_ref, *, kd, vd, scale):
    nl = _norm_mod(h_ref[...], g1_ref[...], mod_ref[0, 0:1, :], mod_ref[0, 1:2, :]).astype(BF16)
    q_ref[...] = (_dot(nl, w_ref[:, 0:kd]) * scale).astype(BF16)
    k_ref[...] = _dot(nl, w_ref[:, kd:2 * kd]).astype(BF16)
    v_ref[...] = _dot(nl, w_ref[:, 2 * kd:2 * kd + vd]).astype(BF16)
    g_ref[...] = _dot(nl, w_ref[:, 2 * kd + vd:]).astype(BF16)
    a1 = _dot(nl, wa_ref[...])
    z = _dot(a1.astype(BF16), w2_ref[...]) + b2_ref[...]
    la_ref[...] = -(jnp.maximum(-z, 0.0) + jnp.log1p(jnp.exp(-jnp.abs(z)))) * (1.0 / GLA_TAU)


def _odd_in(h, mods, g1, w_main, w_a, w_2, b_2, *, tm, midx, kd, vd, scale):
    r, d = h.shape
    row = lambda i: (i, 0)
    const = lambda i: (0, 0)
    return pl.pallas_call(
        functools.partial(_odd_in_kernel, kd=kd, vd=vd, scale=scale),
        grid=(r // tm,),
        in_specs=[pl.BlockSpec((tm, d), row),
                  pl.BlockSpec((1, 6, d), lambda i: (midx(i), 0, 0)),
                  pl.BlockSpec((1, d), const),
                  pl.BlockSpec(w_main.shape, const),
                  pl.BlockSpec(w_a.shape, const),
                  pl.BlockSpec(w_2.shape, const),
                  pl.BlockSpec(b_2.shape, const)],
        out_specs=[pl.BlockSpec((tm, kd), row), pl.BlockSpec((tm, kd), row), pl.BlockSpec((tm, vd), row),
                   pl.BlockSpec((tm, vd), row), pl.BlockSpec((tm, 2 * kd), row)],
        out_shape=[jax.ShapeDtypeStruct((r, kd), BF16), jax.ShapeDtypeStruct((r, kd), BF16),
                   jax.ShapeDtypeStruct((r, vd), BF16), jax.ShapeDtypeStruct((r, vd), BF16),
                   jax.ShapeDtypeStruct((r, 2 * kd), F32)],
        compiler_params=_params(("arbitrary",)),
        name="odd_in",
    )(h, mods, g1, w_main, w_a, w_2, b_2)


def _gla_kernel(q_ref, k_ref, v_ref, la_ref, tri_ref, o_ref, s_sc, b_sc, q32, o32, *, heads, dk, dv):
    c = q_ref.shape[1]

    @pl.when(pl.program_id(1) == 0)
    def _():
        s_sc[...] = jnp.zeros(s_sc.shape, F32)

    tri = tri_ref[...]
    row = lax.broadcasted_iota(jnp.int32, (c, c), 0)
    col = lax.broadcasted_iota(jnp.int32, (c, c), 1)
    causal = col <= row

    worst = jnp.zeros((1, 1), F32)
    for hh in range(heads):
        la_hi, la_lo = _split_bf16(la_ref[0, :, hh * dk:(hh + 1) * dk])
        b = _dot(tri, la_hi) + _dot(tri, la_lo)
        b_sc[hh] = b
        worst = jnp.maximum(worst, jnp.max(-b[c - 1:c, :], axis=-1, keepdims=True))
    safe = worst[0, 0] <= GLA_SAFE_DECAY

    @pl.when(safe)
    def _():
        for hh in range(heads):
            b = b_sc[hh]
            q = q_ref[0, :, hh * dk:(hh + 1) * dk].astype(F32)
            k = k_ref[0, :, hh * dk:(hh + 1) * dk].astype(F32)
            a = _dot_nt((q * jnp.exp(b)).astype(BF16), (k * jnp.exp(-b)).astype(BF16))
            a = jnp.where(causal, a, 0.0).astype(BF16)
            o32[hh] = _dot(a, v_ref[0, :, hh * dv:(hh + 1) * dv])

    @pl.when(jnp.logical_not(safe))
    def _():
        jrow = lax.broadcasted_iota(jnp.int32, (c, dk), 0)
        for hh in range(heads):
            q32[...] = q_ref[0, :, hh * dk:(hh + 1) * dk].astype(F32)
            kh = k_ref[0, :, hh * dk:(hh + 1) * dk].astype(F32)
            vh = v_ref[0, :, hh * dv:(hh + 1) * dv].astype(F32)
            bh = b_sc[hh]

            def body(i, carry):
                bi = b_sc[hh, pl.ds(i, 1), :]
                qi = q32[pl.ds(i, 1), :]
                dec = jnp.exp(jnp.where(jrow <= i, bi - bh, -jnp.inf))
                a = jnp.sum(qi * kh * dec, axis=-1, keepdims=True)
                o32[hh, pl.ds(i, 1), :] = jnp.sum(a * vh, axis=0, keepdims=True)
                return carry

            lax.fori_loop(0, c, body, 0)

    for hh in range(heads):
        b = b_sc[hh]
        bend = b[c - 1:c, :]
        q = q_ref[0, :, hh * dk:(hh + 1) * dk].astype(F32)
        k = k_ref[0, :, hh * dk:(hh + 1) * dk].astype(F32)
        vh = v_ref[0, :, hh * dv:(hh + 1) * dv]
        s = s_sc[hh]
        o_inter = _dot((q * jnp.exp(b)).astype(BF16), s.astype(BF16))
        o_ref[0, :, hh * dv:(hh + 1) * dv] = (o32[hh] + o_inter).astype(BF16)
        kd_t = (k * jnp.exp(bend - b)).T.astype(BF16)
        bend_col = b.T[:, c - 1:c]
        s_sc[hh] = jnp.exp(bend_col) * s + _dot(kd_t, vh)


def _gla(q, k, v, la, *, heads):
    ns, lt, kdim = q.shape
    vdim = v.shape[2]
    c = GLA_BLOCK
    dk, dv = kdim // heads, vdim // heads
    tri = jnp.tril(jnp.ones((c, c), F32)).astype(BF16)
    blk = lambda n: pl.BlockSpec((1, c, n), lambda s, j: (s, j, 0))
    return pl.pallas_call(
        functools.partial(_gla_kernel, heads=heads, dk=dk, dv=dv),
        grid=(ns, lt // c),
        in_specs=[blk(kdim), blk(kdim), blk(vdim), blk(kdim), pl.BlockSpec((c, c), lambda s, j: (0, 0))],
        out_specs=blk(vdim),
        out_shape=jax.ShapeDtypeStruct((ns, lt, vdim), BF16),
        scratch_shapes=[pltpu.VMEM((heads, dk, dv), F32), pltpu.VMEM((heads, c, dk), F32),
                        pltpu.VMEM((c, dk), F32), pltpu.VMEM((heads, c, dv), F32)],
        compiler_params=_params(("arbitrary", "arbitrary")),
        name="gla_scan",
    )(q, k, v, la, tri)


def _rope_tables(n_lat, hd, reps, pad_rows):
    rows = n_lat // GRID_W
    r, col = jnp.meshgrid(jnp.arange(rows, dtype=F32), jnp.arange(GRID_W, dtype=F32), indexing='ij')
    half = hd // 2
    inv = ROPE_BASE ** (-jnp.arange(0, half, 2, dtype=F32) / half)
    ar = r.reshape(-1, 1) * inv
    ac = col.reshape(-1, 1) * inv
    ang = jnp.concatenate([ar, ar, ac, ac], axis=-1)
    cos, sin = jnp.cos(ang), jnp.sin(ang)
    first = (jnp.arange(hd) % (hd // 2)) < hd // 4
    sa = jnp.where(first, -sin, 0.0)
    sb = jnp.where(first, 0.0, sin)

    def expand(t, fill):
        t = jnp.tile(t, (1, reps))
        return jnp.concatenate([t, jnp.full((pad_rows, t.shape[1]), fill, F32)], axis=0)

    return expand(cos, 1.0), expand(sa, 0.0), expand(sb, 0.0)


def kernel(x, c, ctx, c_ctx, ada_w, ada_b, norm1_g, norm2_g, ab_w_in, conv_w, conv_b, conv_ln_g, conv_ln_b, diff_qnorm_g, diff_knorm_g, diff_lq1, diff_lk1, diff_lq2, diff_lk2, diff_subln_g, ab_w_out, gla_w_in, gla_w_a2, gla_b_a2, gla_norm_g, gla_w_out, rg_w, rg_b, re_w, re_b, moe_w1, moe_w3, moe_w2):
    bsz, n_lat, d = x.shape
    n_ctx = ctx.shape[1]
    depth = ada_w.shape[0]
    assert depth == 2, "layer plan below is the depth-2 block: even mixer layer, then final odd layer"
    assert bsz + 1 <= 8
    r_lat, r_ctx = bsz * n_lat, bsz * n_ctx
    r_all = r_lat + r_ctx
    tm = next(t for t in (512, 256, 128) if n_lat % t == 0 and r_ctx % t == 0)
    tc = next(t for t in (256, 128) if n_lat % t == 0 and n_ctx % t == 0)
    lat_tiles = r_lat // tm
    per_seq = n_lat // tm
    midx = lambda i: jnp.minimum(i // per_seq, bsz)
    tidx = lambda i: jnp.where(i < lat_tiles, lax.rem(i, per_seq), per_seq)

    n_groups, n_experts = rg_w.shape[2], re_w.shape[2]
    route_kw = dict(n_groups=n_groups, per_group=n_experts // n_groups)

    cvec = jnp.zeros((8, d), F32).at[:bsz].set(c).at[bsz].set(c_ctx)
    mods_all = _adaln(cvec, ada_w, ada_b).reshape(depth, 8, 6, d)

    def router_consts(l):
        wr = jnp.zeros((d, LANES), F32).at[:, :n_groups].set(rg_w[l]).at[:, n_groups:n_groups + n_experts].set(re_w[l])
        rb = jnp.zeros((1, LANES), F32).at[0, :n_groups].set(rg_b[l]).at[0, n_groups:n_groups + n_experts].set(re_b[l])
        wr_hi, wr_lo = _split_bf16(wr)
        return norm2_g[l][None, :], wr_hi, wr_lo, rb

    h_all = jnp.concatenate([x.reshape(r_lat, d), ctx.reshape(r_ctx, d)], axis=0)

    l = 0
    mods = mods_all[l]
    conv_dim = conv_w.shape[2]
    hd = diff_qnorm_g.shape[1]
    vd = diff_subln_g.shape[1]
    heads = (ab_w_out.shape[1] - conv_dim) // vd
    qk_dim = heads * 2 * hd
    lambda_init = 0.8 - 0.6 * math.exp(-0.3 * l)
    tables = _rope_tables(n_lat, hd, qk_dim // hd, tm)
    gidx = jnp.arange(qk_dim) // hd
    gm = (gidx[:, None] == gidx[None, :]).astype(F32) * (1.0 / hd)
    qg = jnp.tile(diff_qnorm_g[0], qk_dim // hd)[None, :]
    kg = jnp.tile(diff_knorm_g[0], qk_dim // hd)[None, :]
    hconv, q, k, v = _even_in(h_all, mods, norm1_g[l][None, :], ab_w_in[0].astype(BF16), tables, qg, kg,
                              gm.astype(BF16), tm=tm, midx=midx, tidx=tidx, conv_dim=conv_dim, qk_dim=qk_dim,
                              scale=hd ** -0.5)

    lam = (jnp.exp(jnp.sum(diff_lq1[0] * diff_lk1[0])) - jnp.exp(jnp.sum(diff_lq2[0] * diff_lk2[0]))
           + lambda_init).reshape(1, 1).astype(F32)
    sg = diff_subln_g[0][None, :]
    split = lambda t: (t[:r_lat].reshape(bsz, n_lat, -1), t[r_lat:].reshape(bsz, n_ctx, -1))
    (ql, qc), (kl, kc), (vl, vc) = split(q), split(k), split(v)
    attn_kw = dict(heads=heads, hd=hd, post=1.0 - lambda_init)
    o_l = _attention(ql, jnp.concatenate([kc, kl], axis=1), jnp.concatenate([vc, vl], axis=1), lam, sg, **attn_kw)
    o_c = _attention(qc, kc, vc, lam, sg, **attn_kw)
    attn = jnp.concatenate([o_l.reshape(r_lat, -1), o_c.reshape(r_ctx, -1)], axis=0)

    tiles = jnp.arange(r_all // tc)
    seq_tiles = jnp.where(tiles < r_lat // tc, n_lat // tc, n_ctx // tc)
    in_seq = jnp.where(tiles < r_lat // tc, tiles % (n_lat // tc), (tiles - r_lat // tc) % (n_ctx // tc))
    flags = jnp.stack([in_seq == 0, in_seq == seq_tiles - 1]).astype(jnp.int32)
    conv = _conv(hconv, flags, conv_w[0], conv_b[0][None, :], conv_ln_g[0][None, :], conv_ln_b[0][None, :], tc=tc)

    h_all, xt, route = _call_out(
        functools.partial(_even_out_kernel, conv_dim=conv_dim, route_kw=route_kw),
        [conv, attn], [], h_all, mods, [ab_w_out[0].astype(BF16), *router_consts(l)],
        tm=tm, n_tiles=r_all // tm, midx=midx, name="even_out")
    tm2 = min(tm, 256)
    midx2 = lambda i: jnp.minimum(i // (n_lat // tm2), bsz)
    h_all = _moe(h_all, xt, route, mods, moe_w1[l].astype(BF16), moe_w3[l].astype(BF16), moe_w2[l].astype(BF16),
                 tm=tm2, midx=midx2)

    l = 1
    mods = mods_all[l]
    dv = gla_norm_g.shape[1]
    vdim = gla_w_out.shape[1]
    g_heads = vdim // dv
    rank = gla_w_a2.shape[2]
    kdim = gla_w_a2.shape[3]
    dk = kdim // g_heads
    w_in = gla_w_in[0]
    n_main = 2 * kdim + 2 * vdim
    w_a = jnp.zeros((d, LANES), F32).at[:, :2 * rank].set(w_in[:, n_main:]).astype(BF16)
    w_2 = (jnp.zeros((LANES, 2 * kdim), F32).at[:rank, :kdim].set(gla_w_a2[0, 0])
           .at[rank:2 * rank, kdim:].set(gla_w_a2[0, 1])).astype(BF16)
    b_2 = jnp.concatenate([gla_b_a2[0, 0], gla_b_a2[0, 1]])[None, :]
    gq, gk, gv, gg, la = _odd_in(h_all, mods, norm1_g[l][None, :], w_in[:, :n_main].astype(BF16), w_a, w_2, b_2,
                                 tm=tm, midx=midx, kd=kdim, vd=vdim, scale=dk ** -0.5)

    def streams(t):
        lat, cx = split(t)
        fwd = jnp.concatenate([cx, lat], axis=1)
        bwd = jnp.flip(jnp.concatenate([lat, cx], axis=1), axis=1)
        return fwd, bwd

    (qf, qb), (kf, kb), (vf, vb) = streams(gq), streams(gk), streams(gv)
    laf, _ = streams(la[:, :kdim])
    _, lab = streams(la[:, kdim:])
    stack = lambda a, b_: jnp.concatenate([a, b_], axis=0)
    o = _gla(stack(qf, qb), stack(kf, kb), stack(vf, vb), stack(laf, lab), heads=g_heads)
    o_f = o[:bsz, n_ctx:].reshape(r_lat, vdim)
    o_b = jnp.flip(o[bsz:], axis=1)[:, :n_lat].reshape(r_lat, vdim)

    h_lat, xt, route = _call_out(
        functools.partial(_odd_out_kernel, heads=g_heads, route_kw=route_kw),
        [o_f, o_b, gg], [gla_norm_g[0][None, :]], h_all, mods, [gla_w_out[0].astype(BF16), *router_consts(l)],
        tm=tm, n_tiles=lat_tiles, midx=midx, name="odd_out")
    out = _moe(h_lat, xt, route, mods, moe_w1[l].astype(BF16), moe_w3[l].astype(BF16), moe_w2[l].astype(BF16),
               tm=tm2, midx=midx2)
    return out.reshape(bsz, n_lat, d)
```

```python
import functools
import math

import jax
import jax.numpy as jnp
from jax import lax
from jax.experimental import pallas as pl
from jax.experimental.pallas import tpu as pltpu

F32 = jnp.float32
BF16 = jnp.bfloat16

EPS = 1e-6
GRID_W = 64
ROPE_BASE = 10000.0
GLA_TAU = 16.0
TOP_K = 2
MOE_BLOCK = 256
GLA_BLOCK = 128
GLA_SAFE_DECAY = 60.0
HALO = 16
LANES = 128
VMEM_LIMIT = 56 * 1024 * 1024

_NT = (((1,), (1,)), ((), ()))


def _dot(a, b):
    return jnp.dot(a, b, preferred_element_type=F32)


def _dot_nt(a, b):
    return lax.dot_general(a, b, _NT, preferred_element_type=F32)


def _split_bf16(x):
    hi = x.astype(BF16)
    lo = (x - hi.astype(F32)).astype(BF16)
    return hi, lo


def _sigmoid(x):
    return 1.0 / (1.0 + jnp.exp(-x))


def _silu(x):
    return x * _sigmoid(x)


def _norm_mod(h, g, shift, scale):
    ms = jnp.mean(h * h, axis=-1, keepdims=True)
    return h * lax.rsqrt(ms + EPS) * g * (1.0 + scale) + shift


def _params(sem):
    return pltpu.CompilerParams(dimension_semantics=sem, vmem_limit_bytes=VMEM_LIMIT)


def _adaln_kernel(c_ref, w_ref, b_ref, o_ref):
    s = _silu(c_ref[...])
    s_hi, s_lo = _split_bf16(s)
    w_hi, w_lo = _split_bf16(w_ref[0])
    o_ref[0] = _dot(s_hi, w_hi) + _dot(s_lo, w_hi) + _dot(s_hi, w_lo) + b_ref[0]


def _adaln(cvec, ada_w, ada_b):
    depth, d, n = ada_w.shape
    tn = n // 4
    return pl.pallas_call(
        _adaln_kernel,
        grid=(depth, n // tn),
        in_specs=[pl.BlockSpec((8, d), lambda l, j: (0, 0)),
                  pl.BlockSpec((1, d, tn), lambda l, j: (l, 0, j)),
                  pl.BlockSpec((1, 1, tn), lambda l, j: (l, 0, j))],
        out_specs=pl.BlockSpec((1, 8, tn), lambda l, j: (l, 0, j)),
        out_shape=jax.ShapeDtypeStruct((depth, 8, n), F32),
        compiler_params=_params(("arbitrary", "arbitrary")),
        name="adaln",
    )(cvec, ada_w, ada_b.reshape(depth, 1, n))


def _even_in_kernel(h_ref, mod_ref, g1_ref, w_ref, cos_ref, sa_ref, sb_ref, qg_ref, kg_ref, gm_ref,
                    hc_ref, q_ref, k_ref, v_ref, *, conv_dim, qk_dim, scale):
    nl = _norm_mod(h_ref[...], g1_ref[...], mod_ref[0, 0:1, :], mod_ref[0, 1:2, :]).astype(BF16)
    c, q = conv_dim, qk_dim
    ua = _dot(nl, w_ref[:, 0:c])
    ug = _dot(nl, w_ref[:, c:2 * c])
    hc_ref[...] = ua * _sigmoid(ug)
    gm = gm_ref[...]
    cos, sa, sb = cos_ref[...], sa_ref[...], sb_ref[...]

    def qk_norm_rope(u, g):
        hi, lo = _split_bf16(u * u)
        ms = _dot(hi, gm) + _dot(lo, gm)
        n = u * lax.rsqrt(ms + EPS) * g
        return n * cos + pltpu.roll(n, q - 16, 1) * sa + pltpu.roll(n, 16, 1) * sb

    uq = _dot(nl, w_ref[:, 2 * c:2 * c + q])
    q_ref[...] = (qk_norm_rope(uq, qg_ref[...]) * scale).astype(BF16)
    uk = _dot(nl, w_ref[:, 2 * c + q:2 * c + 2 * q])
    k_ref[...] = qk_norm_rope(uk, kg_ref[...]).astype(BF16)
    v_ref[...] = _dot(nl, w_ref[:, 2 * c + 2 * q:]).astype(BF16)


def _even_in(h, mods, g1, w_in, tables, qg, kg, gm, *, tm, midx, tidx, conv_dim, qk_dim, scale):
    r, d = h.shape
    n_in = w_in.shape[1]
    vd = n_in - 2 * conv_dim - 2 * qk_dim
    cos, sa, sb = tables
    row = lambda i: (i, 0)
    const = lambda i: (0, 0)
    tab = lambda i: (tidx(i), 0)
    return pl.pallas_call(
        functools.partial(_even_in_kernel, conv_dim=conv_dim, qk_dim=qk_dim, scale=scale),
        grid=(r // tm,),
        in_specs=[pl.BlockSpec((tm, d), row),
                  pl.BlockSpec((1, 6, d), lambda i: (midx(i), 0, 0)),
                  pl.BlockSpec((1, d), const),
                  pl.BlockSpec((d, n_in), const),
                  pl.BlockSpec((tm, qk_dim), tab),
                  pl.BlockSpec((tm, qk_dim), tab),
                  pl.BlockSpec((tm, qk_dim), tab),
                  pl.BlockSpec((1, qk_dim), const),
                  pl.BlockSpec((1, qk_dim), const),
                  pl.BlockSpec((qk_dim, qk_dim), const)],
        out_specs=[pl.BlockSpec((tm, conv_dim), row),
                   pl.BlockSpec((tm, qk_dim), row),
                   pl.BlockSpec((tm, qk_dim), row),
                   pl.BlockSpec((tm, vd), row)],
        out_shape=[jax.ShapeDtypeStruct((r, conv_dim), F32),
                   jax.ShapeDtypeStruct((r, qk_dim), BF16),
                   jax.ShapeDtypeStruct((r, qk_dim), BF16),
                   jax.ShapeDtypeStruct((r, vd), BF16)],
        compiler_params=_params(("arbitrary",)),
        name="even_in",
    )(h, mods, g1, w_in, cos, sa, sb, qg, kg, gm)


def _attn_kernel(lam_ref, qt_ref, k_ref, vt_ref, sg_ref, o_ref, s_sc, m_sc, l_sc, acc_sc, *, tk, nk, hd, post):
    qt = qt_ref[0].astype(F32)
    rowi = lax.broadcasted_iota(jnp.int32, qt.shape, 0)
    qm = (jnp.where(rowi < hd, qt, 0.0).astype(BF16), jnp.where(rowi >= hd, qt, 0.0).astype(BF16))
    m_sc[...] = jnp.full(m_sc.shape, -jnp.inf, F32)
    l_sc[...] = jnp.zeros(l_sc.shape, F32)
    acc_sc[...] = jnp.zeros(acc_sc.shape, F32)

    def scores(j, buf):
        off = pl.multiple_of(j * tk, tk)
        kc = k_ref[0, pl.ds(off, tk), :]
        for m in range(2):
            s_sc[buf, m] = _dot(kc, qm[m])

    def softmax_pv(j, buf):
        vtc = vt_ref[0, 0, j]
        for m in range(2):
            s = s_sc[buf, m]
            m_prev = m_sc[m]
            m_new = jnp.maximum(m_prev, jnp.max(s, axis=0, keepdims=True))
            alpha = jnp.exp2(m_prev - m_new)
            p = jnp.exp2(s - m_new)
            l_sc[m] = alpha * l_sc[m] + jnp.sum(p, axis=0, keepdims=True)
            acc_sc[m] = alpha * acc_sc[m] + _dot(vtc, p.astype(BF16))
            m_sc[m] = m_new

    scores(0, 0)

    def body(i, carry):
        j = 2 * i
        scores(j + 1, 1)
        softmax_pv(j, 0)
        scores(jnp.minimum(j + 2, nk - 1), 0)
        softmax_pv(j + 1, 1)
        return carry

    lax.fori_loop(0, nk // 2, body, 0)
    if nk % 2:
        softmax_pv(nk - 1, 0)
    o = acc_sc[0] * (1.0 / l_sc[0]) - lam_ref[...] * (acc_sc[1] * (1.0 / l_sc[1]))
    ms = jnp.mean(o * o, axis=0, keepdims=True)
    o_ref[0] = (o * lax.rsqrt(ms + EPS) * sg_ref[...] * post).astype(BF16)


def _attention(q, k, v, lam, sg, *, heads, hd, post):
    b, lq, _ = q.shape
    lk = k.shape[1]
    vd = v.shape[2] // heads
    tq = 512 if lq % 512 == 0 else lq
    tk = 256 if lk % 256 == 0 else LANES
    nk = lk // tk
    qt = q.transpose(0, 2, 1)
    vt = v.transpose(0, 2, 1).reshape(b, heads, vd, nk, tk).transpose(0, 1, 3, 2, 4)
    ot = pl.pallas_call(
        functools.partial(_attn_kernel, tk=tk, nk=nk, hd=hd, post=post),
        grid=(b, heads, lq // tq),
        in_specs=[pl.BlockSpec((1, 1), lambda bi, h, i: (0, 0)),
                  pl.BlockSpec((1, 2 * hd, tq), lambda bi, h, i: (bi, h, i)),
                  pl.BlockSpec((1, lk, 2 * hd), lambda bi, h, i: (bi, 0, h)),
                  pl.BlockSpec((1, 1, nk, vd, tk), lambda bi, h, i: (bi, h, 0, 0, 0)),
                  pl.BlockSpec((vd, 1), lambda bi, h, i: (0, 0))],
        out_specs=pl.BlockSpec((1, vd, tq), lambda bi, h, i: (bi, h, i)),
        out_shape=jax.ShapeDtypeStruct((b, heads * vd, lq), BF16),
        scratch_shapes=[pltpu.VMEM((2, 2, tk, tq), F32), pltpu.VMEM((2, 1, tq), F32), pltpu.VMEM((2, 1, tq), F32),
                        pltpu.VMEM((2, vd, tq), F32)],
        compiler_params=_params(("arbitrary", "arbitrary", "arbitrary")),
        name="diff_attn",
    )(lam, qt, k, vt, sg.reshape(vd, 1))
    return ot.transpose(0, 2, 1)


def _conv_kernel(flag_ref, cur_ref, prev_ref, next_ref, w_ref, b_ref, lg_ref, lb_ref, o_ref, buf, *, tc, width):
    i = pl.program_id(0)
    first = flag_ref[0, i] == 1
    last = flag_ref[1, i] == 1
    buf[0:HALO, :] = jnp.where(first, 0.0, prev_ref[...])
    buf[HALO:HALO + tc, :] = cur_ref[...]
    buf[HALO + tc:, :] = jnp.where(last, 0.0, next_ref[...])
    base = HALO - width // 2
    acc = buf[pl.ds(base, tc), :] * w_ref[0:1, :]
    for t in range(1, width):
        acc = acc + buf[pl.ds(base + t, tc), :] * w_ref[t:t + 1, :]
    acc = acc + b_ref[...]
    mu = jnp.mean(acc, axis=-1, keepdims=True)
    xc = acc - mu
    var = jnp.mean(xc * xc, axis=-1, keepdims=True)
    y = xc * lax.rsqrt(var + EPS) * lg_ref[...] + lb_ref[...]
    o_ref[...] = _silu(y).astype(BF16)


def _conv(hc, flags, w, b, lg, lb, *, tc):
    r, c = hc.shape
    width = w.shape[0]
    nh = tc // HALO
    grid_spec = pltpu.PrefetchScalarGridSpec(
        num_scalar_prefetch=1,
        grid=(r // tc,),
        in_specs=[pl.BlockSpec((tc, c), lambda i, f: (i, 0)),
                  pl.BlockSpec((HALO, c), lambda i, f: (jnp.maximum(i * nh - 1, 0), 0)),
                  pl.BlockSpec((HALO, c), lambda i, f: (jnp.minimum((i + 1) * nh, r // HALO - 1), 0)),
                  pl.BlockSpec((width, c), lambda i, f: (0, 0)),
                  pl.BlockSpec((1, c), lambda i, f: (0, 0)),
                  pl.BlockSpec((1, c), lambda i, f: (0, 0)),
                  pl.BlockSpec((1, c), lambda i, f: (0, 0))],
        out_specs=pl.BlockSpec((tc, c), lambda i, f: (i, 0)),
        scratch_shapes=[pltpu.VMEM((tc + 2 * HALO, c), F32)])
    return pl.pallas_call(
        functools.partial(_conv_kernel, tc=tc, width=width),
        grid_spec=grid_spec,
        out_shape=jax.ShapeDtypeStruct((r, c), BF16),
        compiler_params=_params(("arbitrary",)),
        name="conformer_conv",
    )(flags, hc, hc, hc, w, b, lg, lb)


def _route(nl2, wr_hi_ref, wr_lo_ref, rb_ref, *, n_groups, per_group):
    x_hi, x_lo = _split_bf16(nl2)
    wr_hi = wr_hi_ref[...]
    logits = _dot(x_hi, wr_hi) + _dot(x_lo, wr_hi) + _dot(x_hi, wr_lo_ref[...]) + rb_ref[...]
    lane = lax.broadcasted_iota(jnp.int32, logits.shape, 1)
    ninf = -jnp.inf
    big = jnp.int32(1 << 20)
    lg = jnp.where(lane < n_groups, logits, ninf)
    gmax = jnp.max(lg, axis=-1, keepdims=True)
    gi = jnp.min(jnp.where(lg == gmax, lane, big), axis=-1, keepdims=True)
    pg_sel = 1.0 / jnp.sum(jnp.exp(lg - gmax), axis=-1, keepdims=True)
    lo_lane = n_groups + gi * per_group
    le = jnp.where(lane >= lo_lane, jnp.where(lane < lo_lane + per_group, logits, ninf), ninf)
    v1 = jnp.max(le, axis=-1, keepdims=True)
    i1 = jnp.min(jnp.where(le == v1, lane, big), axis=-1, keepdims=True)
    le2 = jnp.where(lane == i1, ninf, le)
    v2 = jnp.max(le2, axis=-1, keepdims=True)
    i2 = jnp.min(jnp.where(le2 == v2, lane, big), axis=-1, keepdims=True)
    t = jnp.exp(v2 - v1)
    g1 = pg_sel / (1.0 + t)
    g2 = pg_sel * t / (1.0 + t)
    e1 = (i1 - n_groups).astype(F32)
    e2 = (i2 - n_groups).astype(F32)
    return jnp.where(lane == 0, e1, jnp.where(lane == 1, e2, jnp.where(lane == 2, g1, jnp.where(lane == 3, g2, 0.0))))


def _finish_tile(y, h_ref, mod_ref, g2_ref, wr_hi_ref, wr_lo_ref, rb_ref, hn_ref, xt_ref, r_ref, route_kw):
    hn = h_ref[...] + mod_ref[0, 2:3, :] * y
    hn_ref[...] = hn
    nl2 = _norm_mod(hn, g2_ref[...], mod_ref[0, 3:4, :], mod_ref[0, 4:5, :])
    xt_ref[...] = nl2
    r_ref[...] = _route(nl2, wr_hi_ref, wr_lo_ref, rb_ref, **route_kw)


def _even_out_kernel(conv_ref, attn_ref, h_ref, mod_ref, wo_ref, g2_ref, wr_hi_ref, wr_lo_ref, rb_ref,
                     hn_ref, xt_ref, r_ref, *, conv_dim, route_kw):
    y = _dot(conv_ref[...], wo_ref[0:conv_dim, :]) + _dot(attn_ref[...], wo_ref[conv_dim:, :])
    _finish_tile(y, h_ref, mod_ref, g2_ref, wr_hi_ref, wr_lo_ref, rb_ref, hn_ref, xt_ref, r_ref, route_kw)


def _odd_out_kernel(of_ref, ob_ref, gate_ref, ng_ref, h_ref, mod_ref, wo_ref, g2_ref, wr_hi_ref, wr_lo_ref, rb_ref,
                    hn_ref, xt_ref, r_ref, *, heads, route_kw):
    o = of_ref[...].astype(F32) + ob_ref[...].astype(F32)
    dv = o.shape[1] // heads
    parts = []
    for hh in range(heads):
        oh = o[:, hh * dv:(hh + 1) * dv]
        ms = jnp.mean(oh * oh, axis=-1, keepdims=True)
        parts.append(oh * lax.rsqrt(ms + EPS) * ng_ref[...])
    on = jnp.concatenate(parts, axis=1) * _silu(gate_ref[...].astype(F32))
    y = _dot(on.astype(BF16), wo_ref[...])
    _finish_tile(y, h_ref, mod_ref, g2_ref, wr_hi_ref, wr_lo_ref, rb_ref, hn_ref, xt_ref, r_ref, route_kw)


def _call_out(kernel_fn, row_acts, pre_consts, h, mods, post_consts, *, tm, n_tiles, midx, name):
    d = h.shape[1]
    r_out = n_tiles * tm
    row = lambda i: (i, 0)
    const2 = lambda i: (0, 0)
    in_specs = ([pl.BlockSpec((tm, a.shape[1]), row) for a in row_acts]
                + [pl.BlockSpec(a.shape, const2) for a in pre_consts]
                + [pl.BlockSpec((tm, d), row), pl.BlockSpec((1, 6, d), lambda i: (midx(i), 0, 0))]
                + [pl.BlockSpec(a.shape, const2) for a in post_consts])
    return pl.pallas_call(
        kernel_fn,
        grid=(n_tiles,),
        in_specs=in_specs,
        out_specs=[pl.BlockSpec((tm, d), row), pl.BlockSpec((tm, d), row), pl.BlockSpec((tm, LANES), row)],
        out_shape=[jax.ShapeDtypeStruct((r_out, d), F32), jax.ShapeDtypeStruct((r_out, d), F32),
                   jax.ShapeDtypeStruct((r_out, LANES), F32)],
        compiler_params=_params(("arbitrary",)),
        name=name,
    )(*row_acts, *pre_consts, h, mods, *post_consts)


def _gather_rows(idx_ref, n, src_hbm, dst, sem):
    def body(r, carry):
        pltpu.make_async_copy(src_hbm.at[pl.ds(idx_ref[0, 0, r], 1)], dst.at[pl.ds(r, 1)], sem).start()
        return carry
    lax.fori_loop(0, n, body, 0)


def _wait_rows(n, src_hbm, dst, sem):
    pltpu.make_async_copy(src_hbm.at[pl.ds(0, n)], dst, sem).wait()


def _expert_kernel(bexp_ref, tok_ref, tokn_ref, xt_hbm, w1_ref, w3_ref, w2_ref, y_ref, xbuf, sem, *, blk):
    i = pl.program_id(0)
    n = pl.num_programs(0)
    slot = lax.rem(i, 2)

    @pl.when(i == 0)
    def _():
        _gather_rows(tok_ref, blk, xt_hbm, xbuf.at[0], sem.at[0])

    @pl.when(i + 1 < n)
    def _():
        _gather_rows(tokn_ref, blk, xt_hbm, xbuf.at[1 - slot], sem.at[1 - slot])

    _wait_rows(blk, xt_hbm, xbuf.at[slot], sem.at[slot])
    xb = xbuf[slot].astype(BF16)
    hid = _silu(_dot(xb, w1_ref[0])) * _dot(xb, w3_ref[0])
    y_ref[...] = _dot(hid.astype(BF16), w2_ref[0])


def _experts(blk_exp, row_tok, xt, w1, w3, w2):
    n_blk = blk_exp.shape[0]
    blk = MOE_BLOCK
    d = xt.shape[1]
    de = w1.shape[2]
    tok3 = row_tok.reshape(n_blk, 1, blk)
    grid_spec = pltpu.PrefetchScalarGridSpec(
        num_scalar_prefetch=1,
        grid=(n_blk,),
        in_specs=[pl.BlockSpec((1, 1, blk), lambda i, be: (i, 0, 0), memory_space=pltpu.SMEM),
                  pl.BlockSpec((1, 1, blk), lambda i, be: (jnp.minimum(i + 1, n_blk - 1), 0, 0),
                               memory_space=pltpu.SMEM),
                  pl.BlockSpec(memory_space=pl.ANY),
                  pl.BlockSpec((1, d, de), lambda i, be: (be[i], 0, 0)),
                  pl.BlockSpec((1, d, de), lambda i, be: (be[i], 0, 0)),
                  pl.BlockSpec((1, de, d), lambda i, be: (be[i], 0, 0))],
        out_specs=pl.BlockSpec((blk, d), lambda i, be: (i, 0)),
        scratch_shapes=[pltpu.VMEM((2, blk, d), F32), pltpu.SemaphoreType.DMA((2,))])
    return pl.pallas_call(
        functools.partial(_expert_kernel, blk=blk),
        grid_spec=grid_spec,
        out_shape=jax.ShapeDtypeStruct((n_blk * blk, d), F32),
        compiler_params=_params(("arbitrary",)),
        name="moe_experts",
    )(blk_exp, tok3, tok3, xt, w1, w3, w2)


def _combine_kernel(pos_ref, posn_ref, y_hbm, h_ref, r_ref, mod_ref, o_ref, ybuf, sem, *, tm):
    i = pl.program_id(0)
    n = pl.num_programs(0)
    slot = lax.rem(i, 2)

    @pl.when(i == 0)
    def _():
        _gather_rows(pos_ref, 2 * tm, y_hbm, ybuf.at[0], sem.at[0])

    @pl.when(i + 1 < n)
    def _():
        _gather_rows(posn_ref, 2 * tm, y_hbm, ybuf.at[1 - slot], sem.at[1 - slot])

    _wait_rows(2 * tm, y_hbm, ybuf.at[slot], sem.at[slot])
    r = r_ref[...]
    y = r[:, 2:3] * ybuf[slot, 0:tm, :] + r[:, 3:4] * ybuf[slot, tm:, :]
    o_ref[...] = h_ref[...] + mod_ref[0, 5:6, :] * y


def _combine(pos, y_rows, h, route, mods, *, tm, midx):
    r, d = h.shape
    n_t = r // tm
    pos3 = pos.reshape(n_t, tm, TOP_K).transpose(0, 2, 1).reshape(n_t, 1, TOP_K * tm)
    return pl.pallas_call(
        functools.partial(_combine_kernel, tm=tm),
        grid=(n_t,),
        in_specs=[pl.BlockSpec((1, 1, 2 * tm), lambda i: (i, 0, 0), memory_space=pltpu.SMEM),
                  pl.BlockSpec((1, 1, 2 * tm), lambda i: (jnp.minimum(i + 1, n_t - 1), 0, 0),
                               memory_space=pltpu.SMEM),
                  pl.BlockSpec(memory_space=pl.ANY),
                  pl.BlockSpec((tm, d), lambda i: (i, 0)),
                  pl.BlockSpec((tm, LANES), lambda i: (i, 0)),
                  pl.BlockSpec((1, 6, d), lambda i: (midx(i), 0, 0))],
        out_specs=pl.BlockSpec((tm, d), lambda i: (i, 0)),
        out_shape=jax.ShapeDtypeStruct((r, d), F32),
        scratch_shapes=[pltpu.VMEM((2, 2 * tm, d), F32), pltpu.SemaphoreType.DMA((2,))],
        compiler_params=_params(("arbitrary",)),
        name="moe_combine",
    )(pos3, pos3, y_rows, h, route, mods)


def _dispatch(route, n_experts):
    n_tok = route.shape[0]
    n_slot = n_tok * TOP_K
    e_flat = route[:, 0:TOP_K].astype(jnp.int32).reshape(-1)
    order = jnp.argsort(e_flat)
    se = e_flat[order]
    counts = jnp.bincount(e_flat, length=n_experts)
    offs = jnp.cumsum(counts) - counts
    pcounts = (counts + MOE_BLOCK - 1) // MOE_BLOCK * MOE_BLOCK
    pends = jnp.cumsum(pcounts)
    poffs = pends - pcounts
    dest = (poffs[se] + jnp.arange(n_slot, dtype=jnp.int32) - offs[se]).astype(jnp.int32)
    n_blk = (n_slot + n_experts * (MOE_BLOCK - 1) + MOE_BLOCK - 1) // MOE_BLOCK
    tok = (order // TOP_K).astype(jnp.int32)
    row_tok = jnp.zeros((n_blk * MOE_BLOCK,), jnp.int32).at[dest].set(tok)
    pos = jnp.zeros((n_slot,), jnp.int32).at[order].set(dest)
    blk_exp = jnp.minimum(jnp.searchsorted(pends, jnp.arange(n_blk, dtype=jnp.int32) * MOE_BLOCK, side='right'),
                          n_experts - 1).astype(jnp.int32)
    return blk_exp, row_tok, pos


def _moe(h, xt, route, mods, w1, w3, w2, *, tm, midx):
    blk_exp, row_tok, pos = _dispatch(route, w1.shape[0])
    y_rows = _experts(blk_exp, row_tok, xt, w1, w3, w2)
    return _combine(pos, y_rows, h, route, mods, tm=tm, midx=midx)


def _odd_in_kernel(h_ref, mod_ref, g1_ref, w_ref, wa_ref, w2_ref, b2_ref,
                   q_ref, k_ref, v_ref, g_ref, la_ref, *, kd, vd, scale):
    nl = _norm_mod(h_ref[...], g1_ref[...], mod_ref[0, 0:1, :], mod_ref[0, 1:2, :]).astype(BF16)
    q_ref[...] = (_dot(nl, w_ref[:, 0:kd]) * scale).astype(BF16)
    k_ref[...] = _dot(nl, w_ref[:, kd:2 * kd]).astype(BF16)
    v_ref[...] = _dot(nl, w_ref[:, 2 * kd:2 * kd + vd]).astype(BF16)
    g_ref[...] = _dot(nl, w_ref[:, 2 * kd + vd:]).astype(BF16)
    a1 = _dot(nl, wa_ref[...])
    z = _dot(a1.astype(BF16), w2_ref[...]) + b2_ref[...]
    la_ref[...] = -(jnp.maximum(-z, 0.0) + jnp.log1p(jnp.exp(-jnp.abs(z)))) * (1.0 / GLA_TAU)


def _odd_in(h, mods, g1, w_main, w_a, w_2, b_2, *, tm, midx, kd, vd, scale):
    r, d = h.shape
    row = lambda i: (i, 0)
    const = lambda i: (0, 0)
    return pl.pallas_call(
        functools.partial(_odd_in_kernel, kd=kd, vd=vd, scale=scale),
        grid=(r // tm,),
        in_specs=[pl.BlockSpec((tm, d), row),
                  pl.BlockSpec((1, 6, d), lambda i: (midx(i), 0, 0)),
                  pl.BlockSpec((1, d), const),
                  pl.BlockSpec(w_main.shape, const),
                  pl.BlockSpec(w_a.shape, const),
                  pl.BlockSpec(w_2.shape, const),
                  pl.BlockSpec(b_2.shape, const)],
        out_specs=[pl.BlockSpec((tm, kd), row), pl.BlockSpec((tm, kd), row), pl.BlockSpec((tm, vd), row),
                   pl.BlockSpec((tm, vd), row), pl.BlockSpec((tm, 2 * kd), row)],
        out_shape=[jax.ShapeDtypeStruct((r, kd), BF16), jax.ShapeDtypeStruct((r, kd), BF16),
                   jax.ShapeDtypeStruct((r, vd), BF16), jax.ShapeDtypeStruct((r, vd), BF16),
                   jax.ShapeDtypeStruct((r, 2 * kd), F32)],
        compiler_params=_params(("arbitrary",)),
        name="odd_in",
    )(h, mods, g1, w_main, w_a, w_2, b_2)


def _gla_kernel(q_ref, k_ref, v_ref, la_ref, tri_ref, o_ref, s_sc, b_sc, q32, o32, *, heads, dk, dv):
    c = q_ref.shape[1]

    @pl.when(pl.program_id(1) == 0)
    def _():
        s_sc[...] = jnp.zeros(s_sc.shape, F32)

    tri = tri_ref[...]
    row = lax.broadcasted_iota(jnp.int32, (c, c), 0)
    col = lax.broadcasted_iota(jnp.int32, (c, c), 1)
    causal = col <= row

    worst = jnp.zeros((1, 1), F32)
    for hh in range(heads):
        la_hi, la_lo = _split_bf16(la_ref[0, :, hh * dk:(hh + 1) * dk])
        b = _dot(tri, la_hi) + _dot(tri, la_lo)
        b_sc[hh] = b
        worst = jnp.maximum(worst, jnp.max(-b[c - 1:c, :], axis=-1, keepdims=True))
    safe = worst[0, 0] <= GLA_SAFE_DECAY

    @pl.when(safe)
    def _():
        for hh in range(heads):
            b = b_sc[hh]
            q = q_ref[0, :, hh * dk:(hh + 1) * dk].astype(F32)
            k = k_ref[0, :, hh * dk:(hh + 1) * dk].astype(F32)
            a = _dot_nt((q * jnp.exp(b)).astype(BF16), (k * jnp.exp(-b)).astype(BF16))
            a = jnp.where(causal, a, 0.0).astype(BF16)
            o32[hh] = _dot(a, v_ref[0, :, hh * dv:(hh + 1) * dv])

    @pl.when(jnp.logical_not(safe))
    def _():
        jrow = lax.broadcasted_iota(jnp.int32, (c, dk), 0)
        for hh in range(heads):
            q32[...] = q_ref[0, :, hh * dk:(hh + 1) * dk].astype(F32)
            kh = k_ref[0, :, hh * dk:(hh + 1) * dk].astype(F32)
            vh = v_ref[0, :, hh * dv:(hh + 1) * dv].astype(F32)
            bh = b_sc[hh]

            def body(i, carry):
                bi = b_sc[hh, pl.ds(i, 1), :]
                qi = q32[pl.ds(i, 1), :]
                dec = jnp.exp(jnp.where(jrow <= i, bi - bh, -jnp.inf))
                a = jnp.sum(qi * kh * dec, axis=-1, keepdims=True)
                o32[hh, pl.ds(i, 1), :] = jnp.sum(a * vh, axis=0, keepdims=True)
                return carry

            lax.fori_loop(0, c, body, 0)

    for hh in range(heads):
        b = b_sc[hh]
        bend = b[c - 1:c, :]
        q = q_ref[0, :, hh * dk:(hh + 1) * dk].astype(F32)
        k = k_ref[0, :, hh * dk:(hh + 1) * dk].astype(F32)
        vh = v_ref[0, :, hh * dv:(hh + 1) * dv]
        s = s_sc[hh]
        o_inter = _dot((q * jnp.exp(b)).astype(BF16), s.astype(BF16))
        o_ref[0, :, hh * dv:(hh + 1) * dv] = (o32[hh] + o_inter).astype(BF16)
        kd_t = (k * jnp.exp(bend - b)).T.astype(BF16)
        bend_col = b.T[:, c - 1:c]
        s_sc[hh] = jnp.exp(bend_col) * s + _dot(kd_t, vh)


def _gla(q, k, v, la, *, heads):
    ns, lt, kdim = q.shape
    vdim = v.shape[2]
    c = GLA_BLOCK
    dk, dv = kdim // heads, vdim // heads
    tri = jnp.tril(jnp.ones((c, c), F32)).astype(BF16)
    blk = lambda n: pl.BlockSpec((1, c, n), lambda s, j: (s, j, 0))
    return pl.pallas_call(
        functools.partial(_gla_kernel, heads=heads, dk=dk, dv=dv),
        grid=(ns, lt // c),
        in_specs=[blk(kdim), blk(kdim), blk(vdim), blk(kdim), pl.BlockSpec((c, c), lambda s, j: (0, 0))],
        out_specs=blk(vdim),
        out_shape=jax.ShapeDtypeStruct((ns, lt, vdim), BF16),
        scratch_shapes=[pltpu.VMEM((heads, dk, dv), F32), pltpu.VMEM((heads, c, dk), F32),
                        pltpu.VMEM((c, dk), F32), pltpu.VMEM((heads, c, dv), F32)],
        compiler_params=_params(("arbitrary", "arbitrary")),
        name="gla_scan",
    )(q, k, v, la, tri)


def _rope_tables(n_lat, hd, reps, pad_rows):
    rows = n_lat // GRID_W
    r, col = jnp.meshgrid(jnp.arange(rows, dtype=F32), jnp.arange(GRID_W, dtype=F32), indexing='ij')
    half = hd // 2
    inv = ROPE_BASE ** (-jnp.arange(0, half, 2, dtype=F32) / half)
    ar = r.reshape(-1, 1) * inv
    ac = col.reshape(-1, 1) * inv
    ang = jnp.concatenate([ar, ar, ac, ac], axis=-1)
    cos, sin = jnp.cos(ang), jnp.sin(ang)
    first = (jnp.arange(hd) % (hd // 2)) < hd // 4
    sa = jnp.where(first, -sin, 0.0)
    sb = jnp.where(first, 0.0, sin)

    def expand(t, fill):
        t = jnp.tile(t, (1, reps))
        return jnp.concatenate([t, jnp.full((pad_rows, t.shape[1]), fill, F32)], axis=0)

    return expand(cos, 1.0), expand(sa, 0.0), expand(sb, 0.0)


def kernel(x, c, ctx, c_ctx, ada_w, ada_b, norm1_g, norm2_g, ab_w_in, conv_w, conv_b, conv_ln_g, conv_ln_b, diff_qnorm_g, diff_knorm_g, diff_lq1, diff_lk1, diff_lq2, diff_lk2, diff_subln_g, ab_w_out, gla_w_in, gla_w_a2, gla_b_a2, gla_norm_g, gla_w_out, rg_w, rg_b, re_w, re_b, moe_w1, moe_w3, moe_w2):
    bsz, n_lat, d = x.shape
    n_ctx = ctx.shape[1]
    depth = ada_w.shape[0]
    assert depth == 2, "layer plan below is the depth-2 block: even mixer layer, then final odd layer"
    assert bsz + 1 <= 8
    r_lat, r_ctx = bsz * n_lat, bsz * n_ctx
    r_all = r_lat + r_ctx
    tm = next(t for t in (512, 256, 128) if n_lat % t == 0 and r_ctx % t == 0)
    tc = next(t for t in (256, 128) if n_lat % t == 0 and n_ctx % t == 0)
    lat_tiles = r_lat // tm
    per_seq = n_lat // tm
    midx = lambda i: jnp.minimum(i // per_seq, bsz)
    tidx = lambda i: jnp.where(i < lat_tiles, lax.rem(i, per_seq), per_seq)

    n_groups, n_experts = rg_w.shape[2], re_w.shape[2]
    route_kw = dict(n_groups=n_groups, per_group=n_experts // n_groups)

    cvec = jnp.zeros((8, d), F32).at[:bsz].set(c).at[bsz].set(c_ctx)
    mods_all = _adaln(cvec, ada_w, ada_b).reshape(depth, 8, 6, d)

    def router_consts(l):
        wr = jnp.zeros((d, LANES), F32).at[:, :n_groups].set(rg_w[l]).at[:, n_groups:n_groups + n_experts].set(re_w[l])
        rb = jnp.zeros((1, LANES), F32).at[0, :n_groups].set(rg_b[l]).at[0, n_groups:n_groups + n_experts].set(re_b[l])
        wr_hi, wr_lo = _split_bf16(wr)
        return norm2_g[l][None, :], wr_hi, wr_lo, rb

    h_all = jnp.concatenate([x.reshape(r_lat, d), ctx.reshape(r_ctx, d)], axis=0)

    l = 0
    mods = mods_all[l]
    conv_dim = conv_w.shape[2]
    hd = diff_qnorm_g.shape[1]
    vd = diff_subln_g.shape[1]
    heads = (ab_w_out.shape[1] - conv_dim) // vd
    qk_dim = heads * 2 * hd
    lambda_init = 0.8 - 0.6 * math.exp(-0.3 * l)
    tables = _rope_tables(n_lat, hd, qk_dim // hd, tm)
    gidx = jnp.arange(qk_dim) // hd
    gm = (gidx[:, None] == gidx[None, :]).astype(F32) * (1.0 / hd)
    qg = jnp.tile(diff_qnorm_g[0], qk_dim // hd)[None, :]
    kg = jnp.tile(diff_knorm_g[0], qk_dim // hd)[None, :]
    hconv, q, k, v = _even_in(h_all, mods, norm1_g[l][None, :], ab_w_in[0].astype(BF16), tables, qg, kg,
                              gm.astype(BF16), tm=tm, midx=midx, tidx=tidx, conv_dim=conv_dim, qk_dim=qk_dim,
                              scale=hd ** -0.5 * math.log2(math.e))

    lam = (jnp.exp(jnp.sum(diff_lq1[0] * diff_lk1[0])) - jnp.exp(jnp.sum(diff_lq2[0] * diff_lk2[0]))
           + lambda_init).reshape(1, 1).astype(F32)
    sg = diff_subln_g[0][None, :]
    split = lambda t: (t[:r_lat].reshape(bsz, n_lat, -1), t[r_lat:].reshape(bsz, n_ctx, -1))
    (ql, qc), (kl, kc), (vl, vc) = split(q), split(k), split(v)
    attn_kw = dict(heads=heads, hd=hd, post=1.0 - lambda_init)
    o_l = _attention(ql, jnp.concatenate([kc, kl], axis=1), jnp.concatenate([vc, vl], axis=1), lam, sg, **attn_kw)
    o_c = _attention(qc, kc, vc, lam, sg, **attn_kw)
    attn = jnp.concatenate([o_l.reshape(r_lat, -1), o_c.reshape(r_ctx, -1)], axis=0)

    tiles = jnp.arange(r_all // tc)
    seq_tiles = jnp.where(tiles < r_lat // tc, n_lat // tc, n_ctx // tc)
    in_seq = jnp.where(tiles < r_lat // tc, tiles % (n_lat // tc), (tiles - r_lat // tc) % (n_ctx // tc))
    flags = jnp.stack([in_seq == 0, in_seq == seq_tiles - 1]).astype(jnp.int32)
    conv = _conv(hconv, flags, conv_w[0], conv_b[0][None, :], conv_ln_g[0][None, :], conv_ln_b[0][None, :], tc=tc)

    h_all, xt, route = _call_out(
        functools.partial(_even_out_kernel, conv_dim=conv_dim, route_kw=route_kw),
        [conv, attn], [], h_all, mods, [ab_w_out[0].astype(BF16), *router_consts(l)],
        tm=tm, n_tiles=r_all // tm, midx=midx, name="even_out")
    tm2 = min(tm, 256)
    midx2 = lambda i: jnp.minimum(i // (n_lat // tm2), bsz)
    h_all = _moe(h_all, xt, route, mods, moe_w1[l].astype(BF16), moe_w3[l].astype(BF16), moe_w2[l].astype(BF16),
                 tm=tm2, midx=midx2)

    l = 1
    mods = mods_all[l]
    dv = gla_norm_g.shape[1]
    vdim = gla_w_out.shape[1]
    g_heads = vdim // dv
    rank = gla_w_a2.shape[2]
    kdim = gla_w_a2.shape[3]
    dk = kdim // g_heads
    w_in = gla_w_in[0]
    n_main = 2 * kdim + 2 * vdim
    w_a = jnp.zeros((d, LANES), F32).at[:, :2 * rank].set(w_in[:, n_main:]).astype(BF16)
    w_2 = (jnp.zeros((LANES, 2 * kdim), F32).at[:rank, :kdim].set(gla_w_a2[0, 0])
           .at[rank:2 * rank, kdim:].set(gla_w_a2[0, 1])).astype(BF16)
    b_2 = jnp.concatenate([gla_b_a2[0, 0], gla_b_a2[0, 1]])[None, :]
    gq, gk, gv, gg, la = _odd_in(h_all, mods, norm1_g[l][None, :], w_in[:, :n_main].astype(BF16), w_a, w_2, b_2,
                                 tm=tm, midx=midx, kd=kdim, vd=vdim, scale=dk ** -0.5)

    def streams(t):
        lat, cx = split(t)
        fwd = jnp.concatenate([cx, lat], axis=1)
        bwd = jnp.flip(jnp.concatenate([lat, cx], axis=1), axis=1)
        return fwd, bwd

    (qf, qb), (kf, kb), (vf, vb) = streams(gq), streams(gk), streams(gv)
    laf, _ = streams(la[:, :kdim])
    _, lab = streams(la[:, kdim:])
    stack = lambda a, b_: jnp.concatenate([a, b_], axis=0)
    o = _gla(stack(qf, qb), stack(kf, kb), stack(vf, vb), stack(laf, lab), heads=g_heads)
    o_f = o[:bsz, n_ctx:].reshape(r_lat, vdim)
    o_b = jnp.flip(o[bsz:], axis=1)[:, :n_lat].reshape(r_lat, vdim)

    h_lat, xt, route = _call_out(
        functools.partial(_odd_out_kernel, heads=g_heads, route_kw=route_kw),
        [o_f, o_b, gg], [gla_norm_g[0][None, :]], h_all, mods, [gla_w_out[0].astype(BF16), *router_consts(l)],
        tm=tm, n_tiles=lat_tiles, midx=midx, name="odd_out")
    out = _moe(h_lat, xt, route, mods, moe_w1[l].astype(BF16), moe_w3[l].astype(BF16), moe_w2[l].astype(BF16),
               tm=tm2, midx=midx2)
    return out.reshape(bsz, n_lat, d)
```
